```python
import math
import jax, jax.numpy as jnp
from jax import lax
import numpy as np

D_MODEL = 4096
BATCH = 4
SEQ = 4096
DEPTH = 1

CHUNK = 64
EPS = 1e-6
D_MIX = D_MODEL
D_S5 = D_MIX // 2
S5_GROUP = 16
S5_GROUPS = D_S5 // S5_GROUP
S5_STATE = 64
DT_MIN = 1e-3
DT_MAX = 1e-1
D_GM = D_MIX - D_S5
GM_HEADS = 8
GM_HEAD_DIM = D_GM // GM_HEADS
GM_BLOCK = 128
PEER_HEADS = 8
PEER_KEYS = 128
PEER_EXPERTS = PEER_KEYS * PEER_KEYS
PEER_QDIM = 256
PEER_TOPK = 16
PEER_TOKEN_BLOCK = 128

kernel_name = "hymba_s5_gmlp_peer_block"


def _rmsnorm(x, g):
    xf = x.astype(jnp.float32)
    y = xf * lax.rsqrt(jnp.mean(xf * xf, axis=-1, keepdims=True) + EPS)
    return (y * g.astype(jnp.float32)).astype(x.dtype)


def _layernorm(x, g, b):
    xf = x.astype(jnp.float32)
    mu = jnp.mean(xf, axis=-1, keepdims=True)
    xc = xf - mu
    y = xc * lax.rsqrt(jnp.mean(xc * xc, axis=-1, keepdims=True) + EPS)
    return (y * g.astype(jnp.float32) + b.astype(jnp.float32)).astype(x.dtype)


def _complex_affine_combine(left, right):
    a1r, a1i, b1r, b1i = left
    a2r, a2i, b2r, b2i = right
    ar = a2r * a1r - a2i * a1i
    ai = a2r * a1i + a2i * a1r
    br = a2r * b1r - a2i * b1i + b2r
    bi = a2r * b1i + a2i * b1r + b2i
    return (ar, ai, br, bi)


def _s5_mixer(u, a_re, a_im, log_dt, b_re, b_im, c_re, c_im, d_skip, w_glu):
    bsz, seq, _ = u.shape
    uf = u.astype(jnp.float32).reshape(bsz, seq, S5_GROUPS, S5_GROUP)
    dt = jnp.exp(log_dt.astype(jnp.float32))[:, None]
    lr = a_re.astype(jnp.float32)
    li = a_im.astype(jnp.float32)
    mag = jnp.exp(lr * dt)
    ang = li * dt
    abar_r = mag * jnp.cos(ang)
    abar_i = mag * jnp.sin(ang)
    nr = abar_r - 1.0
    ni = abar_i
    den = lr * lr + li * li
    coef_r = (nr * lr + ni * li) / den
    coef_i = (ni * lr - nr * li) / den
    br = b_re.astype(jnp.float32)
    bi = b_im.astype(jnp.float32)
    bbar_r = coef_r[..., None] * br - coef_i[..., None] * bi
    bbar_i = coef_r[..., None] * bi + coef_i[..., None] * br
    bu_r = jnp.einsum('blgc,gpc->lbgp', uf, bbar_r)
    bu_i = jnp.einsum('blgc,gpc->lbgp', uf, bbar_i)
    a_r = jnp.broadcast_to(abar_r[None, None], (seq, 1, S5_GROUPS, S5_STATE))
    a_i = jnp.broadcast_to(abar_i[None, None], (seq, 1, S5_GROUPS, S5_STATE))
    _, _, s_r, s_i = lax.associative_scan(_complex_affine_combine, (a_r, a_i, bu_r, bu_i), axis=0)
    y = (jnp.einsum('lbgp,gcp->blgc', s_r, c_re.astype(jnp.float32))
         - jnp.einsum('lbgp,gcp->blgc', s_i, c_im.astype(jnp.float32)))
    y = (y + d_skip.astype(jnp.float32).reshape(S5_GROUPS, S5_GROUP) * uf).reshape(bsz, seq, D_S5)
    y = jax.nn.gelu(y).astype(u.dtype)
    return y * jax.nn.sigmoid(y @ w_glu)


def _gmlp_mixer(z, ln_g, ln_b, w_s, b_s):
    bsz, seq, _ = z.shape
    z = jax.nn.gelu(z)
    u, v = jnp.split(z, 2, axis=-1)
    v = _layernorm(v, ln_g, ln_b)
    nblk = seq // GM_BLOCK
    v = v.reshape(bsz, nblk, GM_BLOCK, GM_HEADS, GM_HEAD_DIM)
    pos = jnp.arange(GM_BLOCK)
    mask = (pos[None, :] // CHUNK) <= (pos[:, None] // CHUNK)
    w = jnp.where(mask[None], w_s, 0.0)
    gate = jnp.einsum('hij,bnjhd->bnihd', w, v) + jnp.transpose(b_s)[None, None, :, :, None]
    return u * gate.reshape(bsz, seq, D_GM)


def _peer(x, w_q, keys_1, keys_2, expert_down, expert_up):
    bsz, seq, dim = x.shape
    q = (x @ w_q).reshape(bsz, seq, PEER_HEADS, 2, PEER_QDIM // 2)
    s1 = jnp.einsum('blhd,nd->blhn', q[..., 0, :], keys_1).astype(jnp.float32)
    s2 = jnp.einsum('blhd,nd->blhn', q[..., 1, :], keys_2).astype(jnp.float32)
    v1, i1 = lax.top_k(s1, PEER_TOPK)
    v2, i2 = lax.top_k(s2, PEER_TOPK)
    cand = (v1[..., :, None] + v2[..., None, :]).reshape(bsz, seq, PEER_HEADS, PEER_TOPK * PEER_TOPK)
    sc, ci = lax.top_k(cand, PEER_TOPK)
    e1 = jnp.take_along_axis(i1, ci // PEER_TOPK, axis=-1)
    e2 = jnp.take_along_axis(i2, ci % PEER_TOPK, axis=-1)
    expert = e1 * PEER_KEYS + e2
    gates = jax.nn.softmax(sc, axis=-1).astype(x.dtype)
    nblk = (bsz * seq) // PEER_TOKEN_BLOCK
    hk = PEER_HEADS * PEER_TOPK
    xb = x.reshape(nblk, PEER_TOKEN_BLOCK, dim)
    eb = expert.reshape(nblk, PEER_TOKEN_BLOCK, hk)
    gb = gates.reshape(nblk, PEER_TOKEN_BLOCK, hk)

    def block(args):
        xt, et, gt = args
        u = jnp.take(expert_down, et, axis=0)
        act = jax.nn.gelu(jnp.einsum('td,ted->te', xt, u)) * gt
        v = jnp.take(expert_up, et, axis=0)
        return jnp.einsum('te,ted->td', act, v)

    out = lax.map(block, (xb, eb, gb))
    return out.reshape(bsz, seq, dim)


def setup_inputs(seed: int = 0) -> dict:
    key = jax.random.key(seed)
    ks = jax.random.split(key, 27)
    f32 = jnp.float32
    nrm = lambda k, shape, s: jax.random.normal(k, shape, f32) * s
    gain = lambda k, shape: 1.0 + 0.02 * jax.random.normal(k, shape, f32)
    n_idx = jnp.arange(S5_STATE, dtype=f32)
    return {
        "x": nrm(ks[0], (BATCH, SEQ, D_MODEL), 1.0),
        "norm_mix_g": gain(ks[1], (DEPTH, D_MODEL)),
        "w_in": nrm(ks[2], (DEPTH, D_MODEL, D_S5 + 2 * D_GM), D_MODEL ** -0.5),
        "s5_a_re": -0.5 + 0.01 * jax.random.normal(ks[3], (DEPTH, S5_GROUPS, S5_STATE), f32),
        "s5_a_im": math.pi * n_idx + 0.01 * jax.random.normal(ks[4], (DEPTH, S5_GROUPS, S5_STATE), f32),
        "s5_log_dt": jax.random.uniform(ks[5], (DEPTH, S5_GROUPS), f32, math.log(DT_MIN), math.log(DT_MAX)),
        "s5_b_re": nrm(ks[6], (DEPTH, S5_GROUPS, S5_STATE, S5_GROUP), (2 * S5_GROUP) ** -0.5),
        "s5_b_im": nrm(ks[7], (DEPTH, S5_GROUPS, S5_STATE, S5_GROUP), (2 * S5_GROUP) ** -0.5),
        "s5_c_re": nrm(ks[8], (DEPTH, S5_GROUPS, S5_GROUP, S5_STATE), S5_STATE ** -0.5),
        "s5_c_im": nrm(ks[9], (DEPTH, S5_GROUPS, S5_GROUP, S5_STATE), S5_STATE ** -0.5),
        "s5_d": nrm(ks[10], (DEPTH, D_S5), 0.5),
        "s5_w_glu": nrm(ks[11], (DEPTH, D_S5, D_S5), D_S5 ** -0.5),
        "gm_ln_g": gain(ks[12], (DEPTH, D_GM)),
        "gm_ln_b": nrm(ks[13], (DEPTH, D_GM), 0.02),
        "gm_w_s": nrm(ks[14], (DEPTH, GM_HEADS, GM_BLOCK, GM_BLOCK), GM_BLOCK ** -0.5),
        "gm_b_s": gain(ks[15], (DEPTH, GM_HEADS, GM_BLOCK)),
        "norm_s5_out_g": gain(ks[16], (DEPTH, D_S5)),
        "norm_gm_out_g": gain(ks[17], (DEPTH, D_GM)),
        "w_out": nrm(ks[18], (DEPTH, D_MIX, D_MODEL), D_MIX ** -0.5),
        "norm_ffn_g": gain(ks[19], (DEPTH, D_MODEL)),
        "peer_w_q": nrm(ks[20], (DEPTH, D_MODEL, PEER_HEADS * PEER_QDIM), D_MODEL ** -0.5),
        "peer_keys_1": nrm(ks[21], (DEPTH, PEER_KEYS, PEER_QDIM // 2), (PEER_QDIM // 2) ** -0.5),
        "peer_keys_2": nrm(ks[22], (DEPTH, PEER_KEYS, PEER_QDIM // 2), (PEER_QDIM // 2) ** -0.5),
        "peer_down": nrm(ks[23], (DEPTH, PEER_EXPERTS, D_MODEL), D_MODEL ** -0.5),
        "peer_up": nrm(ks[24], (DEPTH, PEER_EXPERTS, D_MODEL), PEER_HEADS ** -0.5),
        "norm_final_g": gain(ks[25], (D_MODEL,)),
    }


def reference(x, norm_mix_g, w_in, s5_a_re, s5_a_im, s5_log_dt, s5_b_re, s5_b_im, s5_c_re, s5_c_im,
              s5_d, s5_w_glu, gm_ln_g, gm_ln_b, gm_w_s, gm_b_s, norm_s5_out_g, norm_gm_out_g, w_out,
              norm_ffn_g, peer_w_q, peer_keys_1, peer_keys_2, peer_down, peer_up, norm_final_g):
    h = x
    for i in range(DEPTH):
        a = _rmsnorm(h, norm_mix_g[i])
        z = a @ w_in[i]
        y_s5 = _s5_mixer(z[..., :D_S5], s5_a_re[i], s5_a_im[i], s5_log_dt[i], s5_b_re[i], s5_b_im[i],
                         s5_c_re[i], s5_c_im[i], s5_d[i], s5_w_glu[i])
        y_gm = _gmlp_mixer(z[..., D_S5:], gm_ln_g[i], gm_ln_b[i], gm_w_s[i], gm_b_s[i])
        mixed = jnp.concatenate([_rmsnorm(y_s5, norm_s5_out_g[i]), _rmsnorm(y_gm, norm_gm_out_g[i])], axis=-1)
        h = h + mixed @ w_out[i]
        h = h + _peer(_rmsnorm(h, norm_ffn_g[i]), peer_w_q[i], peer_keys_1[i], peer_keys_2[i],
                      peer_down[i], peer_up[i])
    return _rmsnorm(h, norm_final_g)
```

```python
import functools
import math

import jax
import jax.numpy as jnp
from jax import lax
from jax.experimental import pallas as pl
from jax.experimental.pallas import tpu as pltpu

F32 = jnp.float32
BF16 = jnp.bfloat16

EPS = 1e-6
S5_GROUP = 16
S5_STATE = 64
S5_BUNDLE = 16
GM_HEADS = 8
GM_BLOCK = 128
GM_CHUNK = 64
PEER_HEADS = 8
PEER_KEYS = 128
PEER_TOPK = 16

VMEM_LIMIT = 56 * 1024 * 1024


def _cparams(*sem):
    return pltpu.CompilerParams(dimension_semantics=sem, vmem_limit_bytes=VMEM_LIMIT)


def _gelu(x):
    return jax.nn.gelu(x, approximate=True)


def _rms_kernel(x_ref, g_ref, o_ref):
    x = x_ref[...].astype(F32)
    r = lax.rsqrt(jnp.mean(x * x, axis=-1, keepdims=True) + EPS)
    o_ref[...] = (x * r * g_ref[...]).astype(o_ref.dtype)


def rmsnorm_cast(x, g, tm=256):
    m, c = x.shape
    return pl.pallas_call(
        _rms_kernel,
        grid=(m // tm,),
        in_specs=[pl.BlockSpec((tm, c), lambda i: (i, 0)),
                  pl.BlockSpec((1, c), lambda i: (0, 0))],
        out_specs=pl.BlockSpec((tm, c), lambda i: (i, 0)),
        out_shape=jax.ShapeDtypeStruct((m, c), BF16),
        compiler_params=_cparams("parallel"),
        name="rmsnorm_cast",
    )(x, g.reshape(1, c))


def _rms_t_kernel(x_ref, g_ref, o_ref, ot_ref):
    x = x_ref[...]
    r = lax.rsqrt(jnp.mean(x * x, axis=-1, keepdims=True) + EPS)
    y = x * r * g_ref[...]
    o_ref[...] = y.astype(o_ref.dtype)
    ot_ref[...] = y.T.astype(ot_ref.dtype)


def rmsnorm_cast_t(x, g, tm=256):
    m, c = x.shape
    return pl.pallas_call(
        _rms_t_kernel,
        grid=(m // tm,),
        in_specs=[pl.BlockSpec((tm, c), lambda i: (i, 0)),
                  pl.BlockSpec((1, c), lambda i: (0, 0))],
        out_specs=[pl.BlockSpec((tm, c), lambda i: (i, 0)),
                   pl.BlockSpec((c, tm), lambda i: (0, i))],
        out_shape=[jax.ShapeDtypeStruct((m, c), BF16),
                   jax.ShapeDtypeStruct((c, m), BF16)],
        compiler_params=_cparams("parallel"),
        name="rmsnorm_cast_t",
    )(x, g.reshape(1, c))


def _mm_kernel(a_ref, b_ref, o_ref, *, act):
    acc = jnp.dot(a_ref[...], b_ref[...], preferred_element_type=F32)
    if act == "gelu":
        acc = _gelu(acc)
    o_ref[...] = acc.astype(o_ref.dtype)


def matmul(a, b, *, tm, tn, act=None, out_dtype=F32, out_shape=None, out_map=None, name="matmul"):
    m, k = a.shape
    _, n = b.shape
    if out_shape is None:
        out_shape = (m, n)
    if out_map is None:
        out_map = lambda i, j: (i, j)
    return pl.pallas_call(
        functools.partial(_mm_kernel, act=act),
        grid=(m // tm, n // tn),
        in_specs=[pl.BlockSpec((tm, k), lambda i, j: (i, 0)),
                  pl.BlockSpec((k, tn), lambda i, j: (0, j))],
        out_specs=pl.BlockSpec((tm, tn), out_map),
        out_shape=jax.ShapeDtypeStruct(out_shape, out_dtype),
        compiler_params=_cparams("parallel", "arbitrary"),
        name=name,
    )(a, b)


def _glu_kernel(a_ref, b_ref, y_ref, o_ref):
    acc = jnp.dot(a_ref[...].astype(BF16), b_ref[...], preferred_element_type=F32)
    y = y_ref[...]
    o_ref[...] = y * jax.nn.sigmoid(acc)


def glu_matmul(y, w, *, tm, tn):
    m, k = y.shape
    n = w.shape[1]
    return pl.pallas_call(
        _glu_kernel,
        grid=(m // tm, n // tn),
        in_specs=[pl.BlockSpec((tm, k), lambda i, j: (i, 0)),
                  pl.BlockSpec((k, tn), lambda i, j: (0, j)),
                  pl.BlockSpec((tm, tn), lambda i, j: (i, j))],
        out_specs=pl.BlockSpec((tm, tn), lambda i, j: (i, j)),
        out_shape=jax.ShapeDtypeStruct((m, n), F32),
        compiler_params=_cparams("parallel", "arbitrary"),
        name="glu_matmul",
    )(y, w, y)


def _outproj_kernel(a1_ref, a2_ref, w1_ref, w2_ref, x_ref, o_ref):
    acc = jnp.dot(a1_ref[...], w1_ref[...], preferred_element_type=F32)
    acc = acc + jnp.dot(a2_ref[...], w2_ref[...], preferred_element_type=F32)
    o_ref[...] = x_ref[...] + acc


def outproj(a1_tm, a2, w1, w2, x, *, batch, seq, tm, tn):
    k1 = w1.shape[0]
    k2 = w2.shape[0]
    n = w1.shape[1]
    per_b = seq // tm
    return pl.pallas_call(
        _outproj_kernel,
        grid=(batch * per_b, n // tn),
        in_specs=[pl.BlockSpec((tm, k1), lambda i, j: (i % per_b, i // per_b)),
                  pl.BlockSpec((tm, k2), lambda i, j: (i, 0)),
                  pl.BlockSpec((k1, tn), lambda i, j: (0, j)),
                  pl.BlockSpec((k2, tn), lambda i, j: (0, j)),
                  pl.BlockSpec((tm, tn), lambda i, j: (i, j))],
        out_specs=pl.BlockSpec((tm, tn), lambda i, j: (i, j)),
        out_shape=jax.ShapeDtypeStruct((batch * seq, n), F32),
        compiler_params=_cparams("parallel", "arbitrary"),
        name="outproj",
    )(a1_tm, a2, w1, w2, x)


def _s5_kernel(u_ref, bd_ref, cbd_ref, a_ref, d_ref, y_ref, s_scr, c_scr, *, batch):
    ti = pl.program_id(1)
    rows = u_ref.shape[0]
    half = s_scr.shape[1] // 2
    assert batch == 4

    @pl.when(ti == 0)
    def _():
        c_scr[...] = jnp.zeros_like(c_scr)

    u = u_ref[...]
    s_scr[...] = jnp.dot(u.astype(BF16), bd_ref[0], preferred_element_type=F32)

    lo = lax.broadcasted_iota(jnp.int32, (8, 128), 0) < batch

    def body(k, _):
        r0 = pl.multiple_of(k * 8, 8)
        for lt in range(half // 128):
            lr = pl.ds(lt * 128, 128)
            li = pl.ds(half + lt * 128, 128)
            xr = s_scr[pl.ds(r0, 8), lr]
            xi = s_scr[pl.ds(r0, 8), li]
            cr = c_scr[0, :, lr]
            ci = c_scr[1, :, lr]
            a12r = a_ref[0, 0, :, lr]
            a12i = a_ref[0, 1, :, lr]
            amr = a_ref[0, 2, :, lr]
            ami = a_ref[0, 3, :, lr]
            rr = pltpu.roll(xr, batch, axis=0)
            ri = pltpu.roll(xi, batch, axis=0)
            sr = xr + (amr * rr - ami * ri) + (a12r * cr - a12i * ci)
            si = xi + (amr * ri + ami * rr) + (a12r * ci + a12i * cr)
            s_scr[pl.ds(r0, 8), lr] = sr
            s_scr[pl.ds(r0, 8), li] = si
            c_scr[0, :, lr] = jnp.where(lo, pltpu.roll(sr, batch, axis=0), sr)
            c_scr[1, :, lr] = jnp.where(lo, pltpu.roll(si, batch, axis=0), si)
        return 0

    lax.fori_loop(0, rows // 8, body, 0)

    y = jnp.dot(s_scr[...].astype(BF16), cbd_ref[0], preferred_element_type=F32)
    y_ref[...] = _gelu(y + d_ref[...] * u)


def s5_scan(u_rows, bd, cbd, acoef, d_skip, *, batch, tl):
    rows_total, ch = u_rows.shape
    nb = bd.shape[0]
    cb = bd.shape[1]
    sw = bd.shape[2]
    rows = tl * batch
    return pl.pallas_call(
        functools.partial(_s5_kernel, batch=batch),
        grid=(nb, rows_total // rows),
        in_specs=[pl.BlockSpec((rows, cb), lambda j, t: (t, j)),
                  pl.BlockSpec((1, cb, sw), lambda j, t: (j, 0, 0)),
                  pl.BlockSpec((1, sw, cb), lambda j, t: (j, 0, 0)),
                  pl.BlockSpec((1, 4, 8, sw // 2), lambda j, t: (j, 0, 0, 0)),
                  pl.BlockSpec((1, cb), lambda j, t: (0, j))],
        out_specs=pl.BlockSpec((rows, cb), lambda j, t: (t, j)),
        out_shape=jax.ShapeDtypeStruct((rows_total, ch), F32),
        scratch_shapes=[pltpu.VMEM((rows, sw), F32),
                        pltpu.VMEM((2, 8, sw // 2), F32)],
        compiler_params=_cparams("arbitrary", "arbitrary"),
        name="s5_scan",
    )(u_rows, bd, cbd, acoef, d_skip.reshape(1, ch))


def _s5_weights(a_re, a_im, log_dt, b_re, b_im, c_re, c_im, batch):
    g, p = a_re.shape
    nb = g // S5_BUNDLE
    dt = jnp.exp(log_dt.astype(F32))[:, None]
    lr = a_re.astype(F32)
    li = a_im.astype(F32)
    mag = jnp.exp(lr * dt)
    ang = li * dt
    abar_r = mag * jnp.cos(ang)
    abar_i = mag * jnp.sin(ang)
    nr = abar_r - 1.0
    ni = abar_i
    den = lr * lr + li * li
    coef_r = (nr * lr + ni * li) / den
    coef_i = (ni * lr - nr * li) / den
    br = b_re.astype(F32)
    bi = b_im.astype(F32)
    bbar_r = coef_r[..., None] * br - coef_i[..., None] * bi
    bbar_i = coef_r[..., None] * bi + coef_i[..., None] * br
    eye = jnp.eye(S5_BUNDLE, dtype=F32)

    def bdiag_in(w):
        w = w.reshape(nb, S5_BUNDLE, p, S5_GROUP)
        return jnp.einsum('jgpc,gh->jgchp', w, eye).reshape(nb, S5_BUNDLE * S5_GROUP, S5_BUNDLE * p)

    def bdiag_out(w):
        w = w.reshape(nb, S5_BUNDLE, S5_GROUP, p)
        return jnp.einsum('jgcp,gh->jgphc', w, eye).reshape(nb, S5_BUNDLE * p, S5_BUNDLE * S5_GROUP)

    bd = jnp.concatenate([bdiag_in(bbar_r), bdiag_in(bbar_i)], axis=-1).astype(BF16)
    cbd = jnp.concatenate([bdiag_out(c_re.astype(F32)), -bdiag_out(c_im.astype(F32))], axis=1).astype(BF16)

    ar = abar_r.reshape(nb, 1, S5_BUNDLE * p)
    ai = abar_i.reshape(nb, 1, S5_BUNDLE * p)
    a2r = ar * ar - ai * ai
    a2i = 2.0 * ar * ai
    zero = jnp.zeros_like(ar)
    rep = lambda first, second: jnp.concatenate(
        [jnp.broadcast_to(first, (nb, batch, first.shape[-1])),
         jnp.broadcast_to(second, (nb, 8 - batch, first.shape[-1]))], axis=1)
    acoef = jnp.stack([rep(ar, a2r), rep(ai, a2i), rep(zero, ar), rep(zero, ai)], axis=1)
    return bd, cbd, acoef


def _gmlp_kernel(z_ref, lng_ref, lnb_ref, ws_ref, bs_ref, g_ref, o_ref, y_scr):
    tb = z_ref.shape[0]
    dgm = z_ref.shape[1] // 2
    hd = dgm // GM_HEADS
    v = z_ref[:, dgm:]
    mu = jnp.mean(v, axis=-1, keepdims=True)
    vc = v - mu
    vn = vc * lax.rsqrt(jnp.mean(vc * vc, axis=-1, keepdims=True) + EPS)
    vn = (vn * lng_ref[...] + lnb_ref[...]).astype(BF16)
    pi = lax.broadcasted_iota(jnp.int32, (GM_BLOCK, GM_BLOCK), 0) // GM_CHUNK
    pj = lax.broadcasted_iota(jnp.int32, (GM_BLOCK, GM_BLOCK), 1) // GM_CHUNK
    causal = pj <= pi
    for h in range(GM_HEADS):
        w = jnp.where(causal, ws_ref[h], 0.0).astype(BF16)
        b = bs_ref[h]
        for s in range(tb // GM_BLOCK):
            rs = slice(s * GM_BLOCK, (s + 1) * GM_BLOCK)
            cs = slice(h * hd, (h + 1) * hd)
            gate = jnp.dot(w, vn[rs, cs], preferred_element_type=F32) + b
            y_scr[rs, cs] = z_ref[rs, cs] * gate
    y = y_scr[...]
    r = lax.rsqrt(jnp.mean(y * y, axis=-1, keepdims=True) + EPS)
    o_ref[...] = (y * r * g_ref[...]).astype(o_ref.dtype)


def gmlp(zg, ln_g, ln_b, w_s, b_s, g_out, *, tb=256):
    t, c2 = zg.shape
    dgm = c2 // 2
    return pl.pallas_call(
        _gmlp_kernel,
        grid=(t // tb,),
        in_specs=[pl.BlockSpec((tb, c2), lambda i: (i, 0)),
                  pl.BlockSpec((1, dgm), lambda i: (0, 0)),
                  pl.BlockSpec((1, dgm), lambda i: (0, 0)),
                  pl.BlockSpec((GM_HEADS, GM_BLOCK, GM_BLOCK), lambda i: (0, 0, 0)),
                  pl.BlockSpec((GM_HEADS, GM_BLOCK, 1), lambda i: (0, 0, 0)),
                  pl.BlockSpec((1, dgm), lambda i: (0, 0))],
        out_specs=pl.BlockSpec((tb, dgm), lambda i: (i, 0)),
        out_shape=jax.ShapeDtypeStruct((t, dgm), BF16),
        scratch_shapes=[pltpu.VMEM((tb, dgm), F32)],
        compiler_params=_cparams("parallel"),
        name="gmlp",
    )(zg, ln_g.reshape(1, dgm), ln_b.reshape(1, dgm), w_s, b_s.reshape(GM_HEADS, GM_BLOCK, 1),
      g_out.reshape(1, dgm))


NEG = -1e30


def _top_values(s, n):
    vals = []
    for _ in range(n):
        m = jnp.max(s, axis=0, keepdims=True)
        vals.append(m)
        s = jnp.where(s == m, NEG, s)
    return vals


def _route_kernel(q_ref, k1_ref, k2_ref, s2_ref, e2_ref, thr_ref, e1_ref, s1_scr, v1_scr, v2_scr):
    nk = PEER_KEYS
    kq = k1_ref.shape[1]
    nv = PEER_TOPK + 1
    dn = (((1,), (1,)), ((), ()))
    for h in range(PEER_HEADS):
        q1 = q_ref[:, (2 * h) * kq:(2 * h + 1) * kq]
        q2 = q_ref[:, (2 * h + 1) * kq:(2 * h + 2) * kq]
        s1 = lax.dot_general(k1_ref[...], q1, dn, precision=lax.Precision.HIGHEST,
                             preferred_element_type=F32)
        s2 = lax.dot_general(k2_ref[...], q2, dn, precision=lax.Precision.HIGHEST,
                             preferred_element_type=F32)
        s1_scr[h] = s1
        s2_ref[h] = s2
        v1 = _top_values(s1, nv)
        v2 = _top_values(s2, nv)
        for a in range(nv):
            v1_scr[a, pl.ds(h, 1), :] = v1[a]
            v2_scr[a, pl.ds(h, 1), :] = v2[a]
    cands = []
    for a in range(nv):
        for b in range(nv):
            if (a + 1) * (b + 1) <= nv:
                cands.append(v1_scr[a] + v2_scr[b])
    sc = []
    for _ in range(nv):
        m = cands[0]
        for c in cands[1:]:
            m = jnp.maximum(m, c)
        sc.append(m)
        cands = [jnp.where(c == m, NEG, c) for c in cands]
    tau = 0.5 * (sc[PEER_TOPK - 1] + sc[PEER_TOPK])
    z = jnp.zeros_like(tau)
    for k in range(PEER_TOPK):
        z = z + jnp.exp(sc[k] - sc[0])
    zinv = 1.0 / z
    m1 = v1_scr[0]
    m2 = v2_scr[0]
    for h in range(PEER_HEADS):
        s1 = s1_scr[h]
        s2 = s2_ref[h]
        thr_ref[h] = tau[h:h + 1, :] - s1
        e1_ref[h] = jnp.exp(s1 - m1[h:h + 1, :]) * zinv[h:h + 1, :]
        e2_ref[h] = jnp.exp(s2 - m2[h:h + 1, :])


def peer_route(q, k1, k2, *, tt=256):
    t, qd = q.shape
    nk, kq = k1.shape
    shp = jax.ShapeDtypeStruct((PEER_HEADS, nk, t), F32)
    ospec = pl.BlockSpec((PEER_HEADS, nk, tt), lambda i: (0, 0, i))
    return pl.pallas_call(
        _route_kernel,
        grid=(t // tt,),
        in_specs=[pl.BlockSpec((tt, qd), lambda i: (i, 0)),
                  pl.BlockSpec((nk, kq), lambda i: (0, 0)),
                  pl.BlockSpec((nk, kq), lambda i: (0, 0))],
        out_specs=[ospec, ospec, ospec, ospec],
        out_shape=[shp, shp, shp, shp],
        scratch_shapes=[pltpu.VMEM((PEER_HEADS, nk, tt), F32),
                        pltpu.VMEM((PEER_TOPK + 1, PEER_HEADS, tt), F32),
                        pltpu.VMEM((PEER_TOPK + 1, PEER_HEADS, tt), F32)],
        compiler_params=_cparams("parallel"),
        name="peer_route",
    )(q, k1, k2)


def _peer_kernel(xt_ref, dn_ref, upt_ref, s2_ref, e2_ref, thr_ref, e1_ref, o_ref, hid_scr, act_scr):
    e = pl.program_id(1)
    te = dn_ref.shape[0]
    nk = PEER_KEYS
    n_i = te // nk

    @pl.when(e == 0)
    def _():
        o_ref[...] = jnp.zeros_like(o_ref)

    hid_scr[...] = jnp.dot(dn_ref[...], xt_ref[...], preferred_element_type=F32)

    def body(ii, _):
        i = e * n_i + ii
        r0 = pl.multiple_of(ii * nk, nk)
        gs = jnp.zeros((nk, xt_ref.shape[1]), F32)
        for h in range(PEER_HEADS):
            thr = thr_ref[h, pl.ds(i, 1), :]
            e1 = e1_ref[h, pl.ds(i, 1), :]
            gs = gs + jnp.where(s2_ref[h] >= thr, e2_ref[h], 0.0) * e1
        act = _gelu(hid_scr[pl.ds(r0, nk), :]) * gs
        act_scr[pl.ds(r0, nk), :] = act.astype(act_scr.dtype)
        return 0

    lax.fori_loop(0, n_i, body, 0)
    o_ref[...] += jnp.dot(upt_ref[...], act_scr[...], preferred_element_type=F32)


def peer_experts(xt, down, upt, s2, e2, thr, e1, *, tm, te):
    d, t = xt.shape
    ne = down.shape[0]
    nk = PEER_KEYS
    once = pl.Buffered(1)
    rspec = pl.BlockSpec((PEER_HEADS, nk, tm), lambda i, e: (0, 0, i), pipeline_mode=once)
    return pl.pallas_call(
        _peer_kernel,
        grid=(t // tm, ne // te),
        in_specs=[pl.BlockSpec((d, tm), lambda i, e: (0, i), pipeline_mode=once),
                  pl.BlockSpec((te, d), lambda i, e: (e, 0)),
                  pl.BlockSpec((d, te), lambda i, e: (0, e)),
                  rspec, rspec, rspec, rspec],
        out_specs=pl.BlockSpec((d, tm), lambda i, e: (0, i)),
        out_shape=jax.ShapeDtypeStruct((d, t), F32),
        scratch_shapes=[pltpu.VMEM((te, tm), F32),
                        pltpu.VMEM((te, tm), BF16)],
        compiler_params=_cparams("parallel", "arbitrary"),
        name="peer_experts",
    )(xt, down, upt, s2, e2, thr, e1)


def _final_kernel(h_ref, pt_ref, g_ref, o_ref):
    x = h_ref[...] + pt_ref[...].T
    r = lax.rsqrt(jnp.mean(x * x, axis=-1, keepdims=True) + EPS)
    o_ref[...] = x * r * g_ref[...]


def final_norm(h, peer_t, g, *, tm=256):
    t, d = h.shape
    return pl.pallas_call(
        _final_kernel,
        grid=(t // tm,),
        in_specs=[pl.BlockSpec((tm, d), lambda i: (i, 0)),
                  pl.BlockSpec((d, tm), lambda i: (0, i)),
                  pl.BlockSpec((1, d), lambda i: (0, 0))],
        out_specs=pl.BlockSpec((tm, d), lambda i: (i, 0)),
        out_shape=jax.ShapeDtypeStruct((t, d), F32),
        compiler_params=_cparams("parallel"),
        name="final_norm",
    )(h, peer_t, g.reshape(1, d))


def _layer(h, batch, seq, norm_mix_g, w_in, s5_a_re, s5_a_im, s5_log_dt, s5_b_re, s5_b_im, s5_c_re,
           s5_c_im, s5_d, s5_w_glu, gm_ln_g, gm_ln_b, gm_w_s, gm_b_s, norm_s5_out_g, norm_gm_out_g,
           w_out, norm_ffn_g, peer_w_q, peer_keys_1, peer_keys_2, peer_down, peer_up,
           tm, tn, s5_tl, peer_tm, peer_te):
    t, d = h.shape
    d_s5 = s5_d.shape[0]
    per_b = seq // tm

    a = rmsnorm_cast(h, norm_mix_g)
    w_in16 = w_in.astype(BF16)
    nj = d_s5 // tn
    u_tm = matmul(a, w_in16[:, :d_s5], tm=tm, tn=tn, out_shape=(seq, batch * d_s5),
                  out_map=lambda i, j: (i % per_b, (i // per_b) * nj + j), name="inproj_s5")
    zg = matmul(a, w_in16[:, d_s5:], tm=tm, tn=tn, act="gelu", name="inproj_gm")

    bd, cbd, acoef = _s5_weights(s5_a_re, s5_a_im, s5_log_dt, s5_b_re, s5_b_im, s5_c_re, s5_c_im, batch)
    y = s5_scan(u_tm.reshape(seq * batch, d_s5), bd, cbd, acoef, s5_d, batch=batch, tl=s5_tl)
    y = glu_matmul(y, s5_w_glu.astype(BF16), tm=tm, tn=tn)
    n_s5 = rmsnorm_cast(y, norm_s5_out_g).reshape(seq, batch * d_s5)

    n_gm = gmlp(zg, gm_ln_g, gm_ln_b, gm_w_s, gm_b_s, norm_gm_out_g)

    w_out16 = w_out.astype(BF16)
    h = outproj(n_s5, n_gm, w_out16[:d_s5], w_out16[d_s5:], h, batch=batch, seq=seq, tm=tm, tn=tn)

    xn, xnt = rmsnorm_cast_t(h, norm_ffn_g)
    q = matmul(xn, peer_w_q.astype(BF16), tm=tm, tn=tn, name="peer_q")
    s2, e2, thr, e1 = peer_route(q, peer_keys_1, peer_keys_2)
    peer_t = peer_experts(xnt, peer_down.astype(BF16), peer_up.astype(BF16).T, s2, e2, thr, e1,
                          tm=peer_tm, te=peer_te)
    return h, peer_t


def _forward(x, norm_mix_g, w_in, s5_a_re, s5_a_im, s5_log_dt, s5_b_re, s5_b_im, s5_c_re, s5_c_im, s5_d,
             s5_w_glu, gm_ln_g, gm_ln_b, gm_w_s, gm_b_s, norm_s5_out_g, norm_gm_out_g, w_out, norm_ffn_g,
             peer_w_q, peer_keys_1, peer_keys_2, peer_down, peer_up, norm_final_g,
             tm=1024, tn=512, s5_tl=128, peer_tm=512, peer_te=512):
    batch, seq, d = x.shape
    depth = w_in.shape[0]
    h = x.reshape(batch * seq, d)
    peer_t = None
    for i in range(depth):
        if peer_t is not None:
            h = h + peer_t.T
        h, peer_t = _layer(h, batch, seq, norm_mix_g[i], w_in[i], s5_a_re[i], s5_a_im[i], s5_log_dt[i],
                           s5_b_re[i], s5_b_im[i], s5_c_re[i], s5_c_im[i], s5_d[i], s5_w_glu[i],
                           gm_ln_g[i], gm_ln_b[i], gm_w_s[i], gm_b_s[i], norm_s5_out_g[i],
                           norm_gm_out_g[i], w_out[i], norm_ffn_g[i], peer_w_q[i], peer_keys_1[i],
                           peer_keys_2[i], peer_down[i], peer_up[i], tm, tn, s5_tl, peer_tm, peer_te)
    out = final_norm(h, peer_t, norm_final_g)
    return out.reshape(batch, seq, d)


def kernel(x, norm_mix_g, w_in, s5_a_re, s5_a_im, s5_log_dt, s5_b_re, s5_b_im, s5_c_re, s5_c_im, s5_d, s5_w_glu, gm_ln_g, gm_ln_b, gm_w_s, gm_b_s, norm_s5_out_g, norm_gm_out_g, w_out, norm_ffn_g, peer_w_q, peer_keys_1, peer_keys_2, peer_down, peer_up, norm_final_g):
    return _forward(x, norm_mix_g, w_in, s5_a_re, s5_a_im, s5_log_dt, s5_b_re, s5_b_im, s5_c_re, s5_c_im,
                    s5_d, s5_w_glu, gm_ln_g, gm_ln_b, gm_w_s, gm_b_s, norm_s5_out_g, norm_gm_out_g, w_out,
                    norm_ffn_g, peer_w_q, peer_keys_1, peer_keys_2, peer_down, peer_up, norm_final_g)
```
